```python
import math, functools
import jax, jax.numpy as jnp
from jax import lax
import numpy as np

D_MODEL = 2048
BATCH = 1
SEQ = 8192
DEPTH = 1
DEC_BATCH = 128
DEC_SEQ = 8
PAST_LEN = 2048
PAGE_SIZE = 128

ATT_WIDTH = D_MODEL // 2
ATT_DV = 128
ATT_HEADS = ATT_WIDTH // ATT_DV
ATT_DK = ATT_DV // 2
ATT_SCALE = ATT_DK ** -0.5
ROT_DIM = ATT_DK // 4
ROPE_THETA = 500000.0
Q_BLOCK = 128
SSM_INNER = D_MODEL - ATT_WIDTH
SSM_HEAD_DIM = 64
SSM_HEADS = SSM_INNER // SSM_HEAD_DIM
SSM_GROUPS = 4
SSM_STATE = 128
SSM_CONV = 4
SSD_CHUNK = 128
XBC = SSM_INNER + 2 * SSM_GROUPS * SSM_STATE
IN_SPLITS = (ATT_WIDTH, 2 * ATT_WIDTH, 3 * ATT_WIDTH, 3 * ATT_WIDTH + SSM_INNER, 3 * ATT_WIDTH + SSM_INNER + XBC)
IN_COLS = 3 * ATT_WIDTH + SSM_INNER + XBC + SSM_HEADS
MEM_TOKENS = 256
MEM_HEADS = 4
MEM_DH = D_MODEL // MEM_HEADS
N_GROUPS = 4
EXPERTS_PER_GROUP = 8
N_EXPERTS = N_GROUPS * EXPERTS_PER_GROUP
TOP_K = 2
D_EXPERT = D_MODEL // 2
MOE_BLOCK = 128
EPS = 1e-6

kernel_name = "hymba_diffattn_ssd_hmoe_step"


def rmsnorm(x, g):
    xf = x.astype(jnp.float32)
    xf = xf * lax.rsqrt(jnp.mean(xf * xf, axis=-1, keepdims=True) + EPS)
    return (xf * g.astype(jnp.float32)).astype(x.dtype)


def rope_partial(t, pos):
    half = ROT_DIM // 2
    inv_freq = ROPE_THETA ** (-jnp.arange(half, dtype=jnp.float32) * (2.0 / ROT_DIM))
    ang = pos.astype(jnp.float32)[:, None] * inv_freq[None, :]
    cos = jnp.cos(ang)[None, :, None, None, :]
    sin = jnp.sin(ang)[None, :, None, None, :]
    x1 = t[..., :half].astype(jnp.float32)
    x2 = t[..., half:ROT_DIM].astype(jnp.float32)
    rot = jnp.concatenate([x1 * cos - x2 * sin, x2 * cos + x1 * sin], axis=-1).astype(t.dtype)
    return jnp.concatenate([rot, t[..., ROT_DIM:]], axis=-1)


def diff_weights(s, lam):
    p = jax.nn.softmax(s, axis=-1)
    return p[:, :, 0] - lam * p[:, :, 1]


def diff_attn_prompt(q, k, v, lam):
    b, S = q.shape[0], q.shape[1]
    kpos = jnp.arange(S)

    def one_block(i):
        qb = lax.dynamic_slice_in_dim(q, i * Q_BLOCK, Q_BLOCK, axis=1)
        s = jnp.einsum('bqhmd,bkhmd->bhmqk', qb, k).astype(jnp.float32) * ATT_SCALE
        qpos = i * Q_BLOCK + jnp.arange(Q_BLOCK)
        s = jnp.where(kpos[None, :] <= qpos[:, None], s, -jnp.inf)
        w = diff_weights(s, lam).astype(v.dtype)
        return jnp.einsum('bhqk,bkhd->bqhd', w, v)

    o = lax.map(one_block, jnp.arange(S // Q_BLOCK))
    return jnp.moveaxis(o, 0, 1).reshape(b, S, ATT_HEADS, ATT_DV)


def diff_attn_sample(q, k, v, lam, k_past, v_past):
    T = q.shape[1]
    P = k_past.shape[1]
    s_past = jnp.einsum('bqhmd,bkhmd->bhmqk', q, k_past).astype(jnp.float32) * ATT_SCALE
    s_new = jnp.einsum('bqhmd,bkhmd->bhmqk', q, k).astype(jnp.float32) * ATT_SCALE
    s_new = jnp.where(jnp.tril(jnp.ones((T, T), bool)), s_new, -jnp.inf)
    w = diff_weights(jnp.concatenate([s_past, s_new], axis=-1), lam).astype(v.dtype)
    return (jnp.einsum('bhqk,bkhd->bqhd', w[..., :P], v_past)
            + jnp.einsum('bhqk,bkhd->bqhd', w[..., P:], v))


def ssd_scan(x, dt, a_head, bm, cm, h0):
    b, L, H, P = x.shape
    G, N = bm.shape[2], bm.shape[3]
    R = H // G
    Q = min(SSD_CHUNK, L)
    nc = -(-L // Q)
    pad = nc * Q - L

    def prep(t):
        t = t.astype(jnp.float32)
        return jnp.pad(t, [(0, 0), (0, pad)] + [(0, 0)] * (t.ndim - 2))

    x = prep(x).reshape(b, nc, Q, G, R, P)
    dt = prep(dt).reshape(b, nc, Q, G, R)
    bm = prep(bm).reshape(b, nc, Q, G, N)
    cm = prep(cm).reshape(b, nc, Q, G, N)
    a = jnp.cumsum(dt * a_head.reshape(G, R), axis=2)
    seg = a[:, :, :, None] - a[:, :, None, :]
    causal = jnp.tril(jnp.ones((Q, Q), bool))[:, :, None, None]
    decay = jnp.exp(jnp.where(causal, seg, -jnp.inf))
    cb = jnp.einsum('bctgn,bcsgn->bctsg', cm, bm)
    w = cb[..., None] * decay * dt[:, :, None]
    y = jnp.einsum('bctsgr,bcsgrp->bctgrp', w, x)
    to_end = jnp.exp(a[:, :, -1:] - a) * dt
    s_chunk = jnp.einsum('bcsgn,bcsgr,bcsgrp->bcgrpn', bm, to_end, x)
    chunk_decay = jnp.exp(a[:, :, -1])

    def step(h, inp):
        dec, s = inp
        return h * dec[..., None, None] + s, h

    h_last, h_in = lax.scan(step, h0.astype(jnp.float32).reshape(b, G, R, P, N),
                            (jnp.moveaxis(chunk_decay, 1, 0), jnp.moveaxis(s_chunk, 1, 0)))
    h_in = jnp.moveaxis(h_in, 0, 1)
    y = y + jnp.einsum('bctgn,bcgrpn,bctgr->bctgrp', cm, h_in, jnp.exp(a))
    y = y.reshape(b, nc * Q, H, P)[:, :L]
    return y, h_last.reshape(b, H, P, N)


def ssd_mixer(z, xbc, dt_raw, conv_state, ssm_state, conv_w, conv_b, dt_bias, a_log, d_skip, g_ssm):
    b, L, _ = xbc.shape
    xpad = jnp.concatenate([conv_state.astype(xbc.dtype), xbc], axis=1)
    new_conv = xpad[:, L:]
    xc = lax.conv_general_dilated(xpad, conv_w.astype(xbc.dtype)[:, None, :], (1,), 'VALID',
                                  dimension_numbers=('NWC', 'WIO', 'NWC'), feature_group_count=XBC)
    xc = jax.nn.silu(xc + conv_b.astype(xc.dtype))
    xs, bm, cm = jnp.split(xc, (SSM_INNER, SSM_INNER + SSM_GROUPS * SSM_STATE), axis=-1)
    xs = xs.reshape(b, L, SSM_HEADS, SSM_HEAD_DIM)
    bm = bm.reshape(b, L, SSM_GROUPS, SSM_STATE)
    cm = cm.reshape(b, L, SSM_GROUPS, SSM_STATE)
    dt = jax.nn.softplus(dt_raw.astype(jnp.float32) + dt_bias.astype(jnp.float32))
    a_head = -jnp.exp(a_log.astype(jnp.float32))
    y, new_ssm = ssd_scan(xs, dt, a_head, bm, cm, ssm_state)
    y = y + d_skip.astype(jnp.float32)[:, None] * xs.astype(jnp.float32)
    y = y.reshape(b, L, SSM_INNER) * jax.nn.silu(z.astype(jnp.float32))
    yg = y.reshape(b, L, SSM_GROUPS, SSM_INNER // SSM_GROUPS)
    yg = yg * lax.rsqrt(jnp.mean(yg * yg, axis=-1, keepdims=True) + EPS)
    y = yg.reshape(b, L, SSM_INNER) * g_ssm.astype(jnp.float32)
    return y.astype(z.dtype), new_conv, new_ssm


def mem_kv(mem, g_mem, w_mk, w_mv):
    b, M, _ = mem.shape
    mn = rmsnorm(mem, g_mem)
    return ((mn @ w_mk).reshape(b, M, MEM_HEADS, MEM_DH),
            (mn @ w_mv).reshape(b, M, MEM_HEADS, MEM_DH))


def mem_attend(h, mem_k, mem_v, w_mq, w_mo):
    b, L, _ = h.shape
    q = (h @ w_mq).reshape(b, L, MEM_HEADS, MEM_DH)
    s = jnp.einsum('blhd,bmhd->bhlm', q, mem_k).astype(jnp.float32) * (MEM_DH ** -0.5)
    p = jax.nn.softmax(s, axis=-1).astype(mem_v.dtype)
    o = jnp.einsum('bhlm,bmhd->blhd', p, mem_v).reshape(b, L, D_MODEL)
    return o @ w_mo


def expert_dispatch(xf, experts, gates, w_gate, w_up, w_down):
    T, D = xf.shape
    A = T * TOP_K
    flat_e = experts.reshape(A)
    flat_tok = jnp.arange(A) // TOP_K
    flat_g = gates.reshape(A)
    order = jnp.argsort(flat_e)
    e_sorted = flat_e[order]
    counts = jnp.zeros((N_EXPERTS,), jnp.int32).at[flat_e].add(1)
    padded = (counts + MOE_BLOCK - 1) // MOE_BLOCK * MOE_BLOCK
    pad_end = jnp.cumsum(padded)
    pad_start = pad_end - padded
    start = jnp.cumsum(counts) - counts
    dest = pad_start[e_sorted] + jnp.arange(A) - start[e_sorted]
    n_blocks = -(-A // MOE_BLOCK) + N_EXPERTS
    rows = n_blocks * MOE_BLOCK
    row_tok = jnp.full((rows,), T, jnp.int32).at[dest].set(flat_tok[order])
    row_gate = jnp.zeros((rows,), flat_g.dtype).at[dest].set(flat_g[order])
    blk_expert = jnp.minimum(jnp.searchsorted(pad_end, jnp.arange(n_blocks) * MOE_BLOCK, side='right'),
                             N_EXPERTS - 1)
    xpad = jnp.concatenate([xf, jnp.zeros((1, D), xf.dtype)], axis=0)
    xb = xpad[row_tok].reshape(n_blocks, MOE_BLOCK, D)

    def expert_block(args):
        xblk, e = args
        hid = jax.nn.silu(xblk @ w_gate[e]) * (xblk @ w_up[e])
        return hid @ w_down[e]

    yb = lax.map(expert_block, (xb, blk_expert)).reshape(rows, D)
    yb = yb * row_gate[:, None].astype(yb.dtype)
    return jnp.zeros((T + 1, D), yb.dtype).at[row_tok].add(yb)[:T]


def hier_moe(h, w_rg, w_re, w_gate, w_up, w_down):
    b, L, D = h.shape
    T = b * L
    xf = h.reshape(T, D)
    tok = jnp.arange(T)
    p_grp = jax.nn.softmax((xf @ w_rg).astype(jnp.float32), axis=-1)
    grp = jnp.argmax(p_grp, axis=-1).astype(jnp.int32)
    gate_grp = p_grp[tok, grp][:, None]
    logit_e = (xf @ w_re).astype(jnp.float32).reshape(T, N_GROUPS, EXPERTS_PER_GROUP)[tok, grp]
    p_e = jax.nn.softmax(logit_e, axis=-1)
    top_p, top_i = lax.top_k(p_e, TOP_K)
    gates = gate_grp * top_p / jnp.sum(top_p, axis=-1, keepdims=True)
    experts = grp[:, None] * EXPERTS_PER_GROUP + top_i.astype(jnp.int32)
    return expert_dispatch(xf, experts, gates, w_gate, w_up, w_down).reshape(b, L, D)


def hybrid_layer(x, pos, attn_core, conv_state, ssm_state, mem_k, mem_v, lw, lam_init):
    b, L, _ = x.shape
    u = rmsnorm(x, lw['g_mix']) @ lw['w_in']
    q, k, v, z, xbc, dt_raw = jnp.split(u, IN_SPLITS, axis=-1)
    q = rope_partial(q.reshape(b, L, ATT_HEADS, 2, ATT_DK), pos)
    k = rope_partial(k.reshape(b, L, ATT_HEADS, 2, ATT_DK), pos)
    v = v.reshape(b, L, ATT_HEADS, ATT_DV)
    lam = (jnp.exp(jnp.sum(lw['lambda_q1'].astype(jnp.float32) * lw['lambda_k1'].astype(jnp.float32)))
           - jnp.exp(jnp.sum(lw['lambda_q2'].astype(jnp.float32) * lw['lambda_k2'].astype(jnp.float32)))
           + lam_init)
    o_att = attn_core(q, k, v, lam)
    o_att = rmsnorm(o_att, lw['g_subln']) * (1.0 - lam_init)
    o_ssm, new_conv, new_ssm = ssd_mixer(z, xbc, dt_raw, conv_state, ssm_state, lw['conv_w'], lw['conv_b'],
                                         lw['dt_bias'], lw['a_log'], lw['d_skip'], lw['g_ssm'])
    mixed = jnp.concatenate([o_att.reshape(b, L, ATT_WIDTH).astype(x.dtype), o_ssm], axis=-1)
    x = x + mixed @ lw['w_out']
    x = x + mem_attend(rmsnorm(x, lw['g_cross']), mem_k, mem_v, lw['w_mq'], lw['w_mo'])
    x = x + hier_moe(rmsnorm(x, lw['g_ffn']), lw['w_router_group'], lw['w_router_expert'],
                     lw['w_gate'], lw['w_up'], lw['w_down'])
    return x, k, v, new_conv, new_ssm


def setup_inputs(seed: int = 0) -> dict:
    key = jax.random.key(seed)
    ks = jax.random.split(key, 40)
    f32 = jnp.float32
    n_pages = PAST_LEN // PAGE_SIZE
    n_used = DEC_BATCH * n_pages
    n_pool = n_used + n_used // 4

    def nrm(k, shape, scale):
        return jax.random.normal(k, shape, f32) * scale

    def gain(k, shape):
        return 1.0 + 0.02 * jax.random.normal(k, shape, f32)

    dt0 = jnp.exp(jax.random.uniform(ks[16], (DEPTH, SSM_HEADS), f32, math.log(1e-3), math.log(1e-1)))
    return {
        'x_prompt': nrm(ks[0], (BATCH, SEQ, D_MODEL), 1.0),
        'x_sample': nrm(ks[1], (DEC_BATCH, DEC_SEQ, D_MODEL), 1.0),
        'mem_prompt': nrm(ks[2], (BATCH, MEM_TOKENS, D_MODEL), 1.0),
        'cache_attn_k': nrm(ks[3], (DEPTH, n_pool, PAGE_SIZE, ATT_HEADS, 2, ATT_DK), 1.0),
        'cache_attn_v': nrm(ks[4], (DEPTH, n_pool, PAGE_SIZE, ATT_HEADS, ATT_DV), 1.0),
        'page_table': jax.random.permutation(ks[5], n_pool)[:n_used].reshape(DEC_BATCH, n_pages).astype(jnp.int32),
        'cache_mem_k': nrm(ks[6], (DEPTH, DEC_BATCH, MEM_TOKENS, MEM_HEADS, MEM_DH), 1.0),
        'cache_mem_v': nrm(ks[7], (DEPTH, DEC_BATCH, MEM_TOKENS, MEM_HEADS, MEM_DH), 1.0),
        'state_conv': nrm(ks[8], (DEPTH, DEC_BATCH, SSM_CONV - 1, XBC), 1.0),
        'state_ssm': nrm(ks[9], (DEPTH, DEC_BATCH, SSM_HEADS, SSM_HEAD_DIM, SSM_STATE), 0.1),
        'g_mix': gain(ks[10], (DEPTH, D_MODEL)),
        'w_in': nrm(ks[11], (DEPTH, D_MODEL, IN_COLS), D_MODEL ** -0.5),
        'lambda_q1': nrm(ks[12], (DEPTH, ATT_DK), 0.1),
        'lambda_k1': nrm(ks[13], (DEPTH, ATT_DK), 0.1),
        'lambda_q2': nrm(ks[14], (DEPTH, ATT_DK), 0.1),
        'lambda_k2': nrm(ks[15], (DEPTH, ATT_DK), 0.1),
        'g_subln': gain(ks[17], (DEPTH, ATT_DV)),
        'conv_w': nrm(ks[18], (DEPTH, SSM_CONV, XBC), SSM_CONV ** -0.5),
        'conv_b': nrm(ks[19], (DEPTH, XBC), 0.02),
        'dt_bias': dt0 + jnp.log(-jnp.expm1(-dt0)),
        'a_log': jnp.log(jax.random.uniform(ks[20], (DEPTH, SSM_HEADS), f32, 1.0, 16.0)),
        'd_skip': gain(ks[21], (DEPTH, SSM_HEADS)),
        'g_ssm': gain(ks[22], (DEPTH, SSM_INNER)),
        'w_out': nrm(ks[23], (DEPTH, D_MODEL, D_MODEL), D_MODEL ** -0.5),
        'g_cross': gain(ks[24], (DEPTH, D_MODEL)),
        'g_mem': gain(ks[25], (DEPTH, D_MODEL)),
        'w_mq': nrm(ks[26], (DEPTH, D_MODEL, D_MODEL), D_MODEL ** -0.5),
        'w_mk': nrm(ks[27], (DEPTH, D_MODEL, D_MODEL), D_MODEL ** -0.5),
        'w_mv': nrm(ks[28], (DEPTH, D_MODEL, D_MODEL), D_MODEL ** -0.5),
        'w_mo': nrm(ks[29], (DEPTH, D_MODEL, D_MODEL), D_MODEL ** -0.5),
        'g_ffn': gain(ks[30], (DEPTH, D_MODEL)),
        'w_router_group': nrm(ks[31], (DEPTH, D_MODEL, N_GROUPS), D_MODEL ** -0.5),
        'w_router_expert': nrm(ks[32], (DEPTH, D_MODEL, N_EXPERTS), D_MODEL ** -0.5),
        'w_gate': nrm(ks[33], (DEPTH, N_EXPERTS, D_MODEL, D_EXPERT), D_MODEL ** -0.5),
        'w_up': nrm(ks[34], (DEPTH, N_EXPERTS, D_MODEL, D_EXPERT), D_MODEL ** -0.5),
        'w_down': nrm(ks[35], (DEPTH, N_EXPERTS, D_EXPERT, D_MODEL), D_EXPERT ** -0.5),
        'g_final': gain(ks[36], (D_MODEL,)),
    }


def reference(x_prompt, x_sample, mem_prompt, cache_attn_k, cache_attn_v, page_table, cache_mem_k, cache_mem_v,
              state_conv, state_ssm, g_mix, w_in, lambda_q1, lambda_k1, lambda_q2, lambda_k2, g_subln, conv_w,
              conv_b, dt_bias, a_log, d_skip, g_ssm, w_out, g_cross, g_mem, w_mq, w_mk, w_mv, w_mo, g_ffn,
              w_router_group, w_router_expert, w_gate, w_up, w_down, g_final):
    b, S, _ = x_prompt.shape
    db, T, _ = x_sample.shape
    past_len = page_table.shape[1] * PAGE_SIZE
    pos_prompt = jnp.arange(S)
    pos_sample = past_len + jnp.arange(T)
    conv0 = jnp.zeros((b, SSM_CONV - 1, XBC), x_prompt.dtype)
    ssm0 = jnp.zeros((b, SSM_HEADS, SSM_HEAD_DIM, SSM_STATE), jnp.float32)
    hp, hs = x_prompt, x_sample
    kp_l, vp_l, ks_l, vs_l, cp_l, cs_l, sp_l, ss_l, mkp_l, mvp_l = [], [], [], [], [], [], [], [], [], []
    for l in range(DEPTH):
        lam_init = 0.8 - 0.6 * math.exp(-0.3 * l)
        lw = dict(g_mix=g_mix[l], w_in=w_in[l], lambda_q1=lambda_q1[l], lambda_k1=lambda_k1[l],
                  lambda_q2=lambda_q2[l], lambda_k2=lambda_k2[l], g_subln=g_subln[l], conv_w=conv_w[l],
                  conv_b=conv_b[l], dt_bias=dt_bias[l], a_log=a_log[l], d_skip=d_skip[l], g_ssm=g_ssm[l],
                  w_out=w_out[l], g_cross=g_cross[l], w_mq=w_mq[l], w_mo=w_mo[l], g_ffn=g_ffn[l],
                  w_router_group=w_router_group[l], w_router_expert=w_router_expert[l],
                  w_gate=w_gate[l], w_up=w_up[l], w_down=w_down[l])
        mk_p, mv_p = mem_kv(mem_prompt, g_mem[l], w_mk[l], w_mv[l])
        hp, kp, vp, cp, sp = hybrid_layer(hp, pos_prompt, diff_attn_prompt, conv0, ssm0, mk_p, mv_p, lw, lam_init)
        k_past = cache_attn_k[l, page_table].reshape(db, past_len, ATT_HEADS, 2, ATT_DK)
        v_past = cache_attn_v[l, page_table].reshape(db, past_len, ATT_HEADS, ATT_DV)
        core = functools.partial(diff_attn_sample, k_past=k_past, v_past=v_past)
        hs, ks_, vs_, cs, ss = hybrid_layer(hs, pos_sample, core, state_conv[l], state_ssm[l],
                                           cache_mem_k[l], cache_mem_v[l], lw, lam_init)
        kp_l.append(kp); vp_l.append(vp); ks_l.append(ks_); vs_l.append(vs_)
        cp_l.append(cp); cs_l.append(cs); sp_l.append(sp); ss_l.append(ss)
        mkp_l.append(mk_p); mvp_l.append(mv_p)
    y_prompt = rmsnorm(hp, g_final)
    y_sample = rmsnorm(hs, g_final)
    return (y_prompt, y_sample, jnp.stack(kp_l), jnp.stack(vp_l), jnp.stack(ks_l), jnp.stack(vs_l),
            jnp.stack(cp_l), jnp.stack(cs_l), jnp.stack(sp_l), jnp.stack(ss_l), jnp.stack(mkp_l), jnp.stack(mvp_l))
```

```python
import functools
import math

import jax
import jax.numpy as jnp
from jax import lax
from jax.experimental import pallas as pl
from jax.experimental.pallas import tpu as pltpu

F32 = jnp.float32
BF16 = jnp.bfloat16

D_MODEL = 2048
PAGE_SIZE = 128
ATT_WIDTH = D_MODEL // 2
ATT_DV = 128
ATT_HEADS = ATT_WIDTH // ATT_DV
ATT_DK = ATT_DV // 2
ATT_SCALE = ATT_DK ** -0.5
ROT_DIM = ATT_DK // 4
ROPE_THETA = 500000.0
SSM_INNER = D_MODEL - ATT_WIDTH
SSM_HEAD_DIM = 64
SSM_HEADS = SSM_INNER // SSM_HEAD_DIM
SSM_GROUPS = 4
SSM_STATE = 128
SSM_CONV = 4
SSD_CHUNK = 128
XBC = SSM_INNER + 2 * SSM_GROUPS * SSM_STATE
IN_MAIN = 3 * ATT_WIDTH + SSM_INNER + XBC
MEM_TOKENS = 256
MEM_HEADS = 4
MEM_DH = D_MODEL // MEM_HEADS
N_GROUPS = 4
EXPERTS_PER_GROUP = 8
N_EXPERTS = N_GROUPS * EXPERTS_PER_GROUP
TOP_K = 2
D_EXPERT = D_MODEL // 2
EPS = 1e-6

LANES = 128
SUBLANES = 8
VMEM_LIMIT_CAP = 56 * 1024 * 1024
MOE_BLOCK = 256
NEG_INF = float("-inf")


def _cparams(semantics, vmem_bytes):
    return pltpu.CompilerParams(dimension_semantics=semantics,
                                vmem_limit_bytes=int(min(max(vmem_bytes, 16 * 1024 * 1024), VMEM_LIMIT_CAP)))


def _nt_dot(a, b):
    return lax.dot_general(a, b, (((1,), (1,)), ((), ())), preferred_element_type=F32)


def _silu(x):
    return x / (1.0 + jnp.exp(-x))


def _rmsnorm_rows(x_ref, g_ref, out_ref, rows, chunk):
    def body(c, carry):
        r = pl.multiple_of(c * chunk, chunk)
        x = x_ref[pl.ds(r, chunk), :]
        ms = jnp.mean(x * x, axis=-1, keepdims=True)
        out_ref[pl.ds(r, chunk), :] = (x * lax.rsqrt(ms + EPS) * g_ref[...]).astype(out_ref.dtype)
        return carry
    lax.fori_loop(0, rows // chunk, body, 0)


def _norm_matmul_kernel(*refs, tm, tn, n_rope_blocks, with_dt):
    if with_dt:
        x_ref, g_ref, w_ref, wdt_ref, cos_ref, sa_ref, sb_ref, u_ref, dt_ref, xn_ref = refs
    else:
        x_ref, g_ref, w_ref, u_ref, xn_ref = refs
    j = pl.program_id(1)

    @pl.when(j == 0)
    def _():
        _rmsnorm_rows(x_ref, g_ref, xn_ref, tm, 64)
        if with_dt:
            dt_ref[...] = jnp.dot(xn_ref[...], wdt_ref[...], preferred_element_type=F32)

    u = jnp.dot(xn_ref[...], w_ref[...], preferred_element_type=F32)
    if n_rope_blocks == 0:
        u_ref[...] = u
    else:
        @pl.when(j < n_rope_blocks)
        def _():
            reps = tn // LANES
            cos = jnp.tile(cos_ref[...], (1, reps))
            sa = jnp.tile(sa_ref[...], (1, reps))
            sb = jnp.tile(sb_ref[...], (1, reps))
            half = ROT_DIM // 2
            u_ref[...] = u * cos + pltpu.roll(u, half, 1) * sa + pltpu.roll(u, tn - half, 1) * sb

        @pl.when(j >= n_rope_blocks)
        def _():
            u_ref[...] = u


def _rope_tables(pos):
    half = ROT_DIM // 2
    inv_freq = ROPE_THETA ** (-jnp.arange(half, dtype=F32) * (2.0 / ROT_DIM))
    ang = pos.astype(F32)[:, None] * inv_freq[None, :]
    cos, sin = jnp.cos(ang), jnp.sin(ang)
    r = pos.shape[0]
    rest = ATT_DK - ROT_DIM
    cos64 = jnp.concatenate([cos, cos, jnp.ones((r, rest), F32)], axis=1)
    sa64 = jnp.concatenate([jnp.zeros((r, half), F32), sin, jnp.zeros((r, rest), F32)], axis=1)
    sb64 = jnp.concatenate([-sin, jnp.zeros((r, half + rest), F32)], axis=1)
    dup = lambda t: jnp.concatenate([t, t], axis=1)
    return dup(cos64), dup(sa64), dup(sb64)


def _in_projection(x, g, w_main, w_dt, pos, tm):
    r = x.shape[0]
    tn = 512
    cos, sa, sb = _rope_tables(pos)
    n_rope = 2 * ATT_WIDTH // tn
    kern = functools.partial(_norm_matmul_kernel, tm=tm, tn=tn, n_rope_blocks=n_rope, with_dt=True)
    vmem = 2 * (tm * D_MODEL * 4 + D_MODEL * tn * 2 + tm * tn * 4 + tm * LANES * 4 * 4) + tm * D_MODEL * 2 \
        + 2 * tm * tn * 4
    return pl.pallas_call(
        kern,
        grid=(r // tm, IN_MAIN // tn),
        in_specs=[
            pl.BlockSpec((tm, D_MODEL), lambda i, j: (i, 0)),
            pl.BlockSpec((1, D_MODEL), lambda i, j: (0, 0)),
            pl.BlockSpec((D_MODEL, tn), lambda i, j: (0, j)),
            pl.BlockSpec((D_MODEL, LANES), lambda i, j: (0, 0)),
            pl.BlockSpec((tm, LANES), lambda i, j: (i, 0)),
            pl.BlockSpec((tm, LANES), lambda i, j: (i, 0)),
            pl.BlockSpec((tm, LANES), lambda i, j: (i, 0)),
        ],
        out_specs=[
            pl.BlockSpec((tm, tn), lambda i, j: (i, j)),
            pl.BlockSpec((tm, LANES), lambda i, j: (i, 0)),
        ],
        out_shape=[jax.ShapeDtypeStruct((r, IN_MAIN), F32), jax.ShapeDtypeStruct((r, LANES), F32)],
        scratch_shapes=[pltpu.VMEM((tm, D_MODEL), BF16)],
        compiler_params=_cparams(("parallel", "arbitrary"), vmem),
        name="in_projection",
    )(x, g, w_main, w_dt, cos, sa, sb)


def _norm_matmul(x, g, w, tm, tn):
    r, n = x.shape[0], w.shape[1]
    kern = functools.partial(_norm_matmul_kernel, tm=tm, tn=tn, n_rope_blocks=0, with_dt=False)
    vmem = 2 * (tm * D_MODEL * 4 + D_MODEL * tn * 2 + tm * tn * 4) + tm * D_MODEL * 2 + tm * tn * 4
    return pl.pallas_call(
        kern,
        grid=(r // tm, n // tn),
        in_specs=[
            pl.BlockSpec((tm, D_MODEL), lambda i, j: (i, 0)),
            pl.BlockSpec((1, D_MODEL), lambda i, j: (0, 0)),
            pl.BlockSpec((D_MODEL, tn), lambda i, j: (0, j)),
        ],
        out_specs=pl.BlockSpec((tm, tn), lambda i, j: (i, j)),
        out_shape=jax.ShapeDtypeStruct((r, n), F32),
        scratch_shapes=[pltpu.VMEM((tm, D_MODEL), BF16)],
        compiler_params=_cparams(("parallel", "arbitrary"), vmem),
        name="norm_matmul",
    )(x, g, w)


def _lambda_value(lq1_ref, lk1_ref, lq2_ref, lk2_ref, lam_init):
    s1 = jnp.sum(lq1_ref[...] * lk1_ref[...], axis=-1, keepdims=True)
    s2 = jnp.sum(lq2_ref[...] * lk2_ref[...], axis=-1, keepdims=True)
    return jnp.exp(s1) - jnp.exp(s2) + lam_init


def _online_softmax_step(s, m_ref, l_ref, acc_ref, pv_fn):
    cols = s.shape[1]
    m_prev = m_ref[...]
    m_new = jnp.maximum(m_prev, jnp.max(s, axis=1, keepdims=True))
    alpha = jnp.exp(m_prev - m_new)
    p = jnp.exp(s - jnp.tile(m_new, (1, cols // LANES)))
    l_ref[...] = alpha * l_ref[...] + jnp.sum(p, axis=1, keepdims=True)
    acc_ref[...] = acc_ref[...] * alpha + pv_fn(p.astype(BF16))
    m_ref[...] = m_new


def _diff_combine_subln(o1, o2, lam, gsub_ref, lam_init):
    d = o1 - lam * o2
    ms = jnp.mean(d * d, axis=-1, keepdims=True)
    return d * lax.rsqrt(ms + EPS) * gsub_ref[...] * (1.0 - lam_init)


def _attn_prompt_kernel(lq1_ref, lk1_ref, lq2_ref, lk2_ref, gsub_ref, q_ref, k_ref, v_ref, o_ref,
                        kb_ref, vb_ref, q2_ref, m_ref, l_ref, acc_ref, *, tq, seq, lam_init):
    qi = pl.program_id(1)
    cast_rows = 512

    @pl.when(qi == 0)
    def _():
        def body(c, carry):
            r = pl.multiple_of(c * cast_rows, cast_rows)
            kb_ref[pl.ds(r, cast_rows), :] = k_ref[pl.ds(r, cast_rows), :].astype(BF16)
            vb_ref[pl.ds(r, cast_rows), :] = v_ref[pl.ds(r, cast_rows), :].astype(BF16)
            return carry
        lax.fori_loop(0, seq // cast_rows, body, 0)

    q = q_ref[...] * ATT_SCALE
    lane = lax.broadcasted_iota(jnp.int32, (tq, ATT_DV), 1)
    q2_ref[0:tq, :] = jnp.where(lane < ATT_DK, q, 0.0).astype(BF16)
    q2_ref[tq:2 * tq, :] = jnp.where(lane >= ATT_DK, q, 0.0).astype(BF16)
    m_ref[...] = jnp.full(m_ref.shape, NEG_INF, F32)
    l_ref[...] = jnp.zeros(l_ref.shape, F32)
    acc_ref[...] = jnp.zeros(acc_ref.shape, F32)

    def step(j, masked):
        r = pl.multiple_of(j * tq, tq)
        kb = kb_ref[pl.ds(r, tq), :]
        s = _nt_dot(q2_ref[...], kb)
        if masked:
            row = lax.broadcasted_iota(jnp.int32, (2 * tq, tq), 0)
            col = lax.broadcasted_iota(jnp.int32, (2 * tq, tq), 1)
            row = jnp.where(row >= tq, row - tq, row)
            s = jnp.where(col <= row, s, NEG_INF)
        vb = vb_ref[pl.ds(r, tq), :]
        _online_softmax_step(s, m_ref, l_ref, acc_ref,
                             lambda p: jnp.dot(p, vb, preferred_element_type=F32))

    def body(j, carry):
        step(j, False)
        return carry
    lax.fori_loop(0, qi, body, 0)
    step(qi, True)

    o = acc_ref[...] / l_ref[...]
    lam = _lambda_value(lq1_ref, lk1_ref, lq2_ref, lk2_ref, lam_init)
    o_ref[...] = _diff_combine_subln(o[0:tq], o[tq:2 * tq], lam, gsub_ref, lam_init).astype(o_ref.dtype)


def _attn_prompt(u, lams, g_subln, lam_init, tq):
    seq = u.shape[0]
    kern = functools.partial(_attn_prompt_kernel, tq=tq, seq=seq, lam_init=lam_init)
    small = pl.BlockSpec((1, ATT_DK), lambda h, i: (0, 0))
    kv_off = ATT_WIDTH // ATT_DV
    vmem = 2 * (2 * seq * ATT_DV * 4 + tq * ATT_DV * 4 + tq * ATT_DV * 2) + 2 * seq * ATT_DV * 2 \
        + 2 * tq * ATT_DV * 2 + 3 * 2 * tq * LANES * 4 + 4 * 2 * tq * tq * 4
    return pl.pallas_call(
        kern,
        grid=(ATT_HEADS, seq // tq),
        in_specs=[small, small, small, small,
                  pl.BlockSpec((1, ATT_DV), lambda h, i: (0, 0)),
                  pl.BlockSpec((tq, ATT_DV), lambda h, i: (i, h)),
                  pl.BlockSpec((seq, ATT_DV), lambda h, i: (0, kv_off + h)),
                  pl.BlockSpec((seq, ATT_DV), lambda h, i: (0, 2 * kv_off + h))],
        out_specs=pl.BlockSpec((tq, ATT_DV), lambda h, i: (i, h)),
        out_shape=jax.ShapeDtypeStruct((seq, ATT_WIDTH), BF16),
        scratch_shapes=[pltpu.VMEM((seq, ATT_DV), BF16), pltpu.VMEM((seq, ATT_DV), BF16),
                        pltpu.VMEM((2 * tq, ATT_DV), BF16),
                        pltpu.VMEM((2 * tq, LANES), F32), pltpu.VMEM((2 * tq, LANES), F32),
                        pltpu.VMEM((2 * tq, ATT_DV), F32)],
        compiler_params=_cparams(("parallel", "arbitrary"), vmem),
        name="attn_prompt",
    )(*lams, g_subln, u, u, u)


def _attn_sample_kernel(pt_ref, lq1_ref, lk1_ref, lq2_ref, lk2_ref, gsub_ref, q_ref, kn_ref, vn_ref,
                        kt_ref, vp_ref, o_ref, qx_ref, m_ref, l_ref, acc_ref, *, t_new, n_pages, lam_init):
    del pt_ref
    p = pl.program_id(1)
    rows = 2 * t_new
    lane = lax.broadcasted_iota(jnp.int32, (t_new, ATT_DV), 1)

    @pl.when(p == 0)
    def _():
        for h in range(ATT_HEADS):
            qh = q_ref[:, h * ATT_DV:(h + 1) * ATT_DV] * ATT_SCALE
            qx_ref[h * rows:h * rows + t_new, :] = jnp.where(lane < ATT_DK, qh, 0.0)
            qx_ref[h * rows + t_new:(h + 1) * rows, :] = jnp.where(lane >= ATT_DK, qh, 0.0)
        m_ref[...] = jnp.full(m_ref.shape, NEG_INF, F32)
        l_ref[...] = jnp.zeros(l_ref.shape, F32)
        acc_ref[...] = jnp.zeros(acc_ref.shape, F32)

    def head_scores(h, k_fn):
        qx = qx_ref[h * rows:(h + 1) * rows, :].astype(BF16)
        return k_fn(qx, h)

    @pl.when(p < n_pages)
    def _():
        def k_fn(qx, h):
            kt = kt_ref[h].reshape(2 * ATT_DK, PAGE_SIZE).astype(BF16)
            return jnp.dot(qx, kt, preferred_element_type=F32)
        s = jnp.concatenate([head_scores(h, k_fn) for h in range(ATT_HEADS)], axis=0)

        def pv(pb):
            outs = []
            for h in range(ATT_HEADS):
                vh = vp_ref[pl.ds(h, PAGE_SIZE, stride=ATT_HEADS), :].astype(BF16)
                outs.append(jnp.dot(pb[h * rows:(h + 1) * rows], vh, preferred_element_type=F32))
            return jnp.concatenate(outs, axis=0)
        _online_softmax_step(s, m_ref, l_ref, acc_ref, pv)

    @pl.when(p == n_pages)
    def _():
        pad = jnp.zeros((PAGE_SIZE - t_new, ATT_WIDTH), F32)
        kpad = jnp.concatenate([kn_ref[...], pad], axis=0).astype(BF16)
        vpad = jnp.concatenate([vn_ref[...], pad], axis=0).astype(BF16)

        def k_fn(qx, h):
            return _nt_dot(qx, kpad[:, h * ATT_DV:(h + 1) * ATT_DV])
        s = jnp.concatenate([head_scores(h, k_fn) for h in range(ATT_HEADS)], axis=0)
        row = lax.broadcasted_iota(jnp.int32, s.shape, 0)
        col = lax.broadcasted_iota(jnp.int32, s.shape, 1)
        s = jnp.where(col <= (row & (t_new - 1)), s, NEG_INF)

        def pv(pb):
            return jnp.concatenate(
                [jnp.dot(pb[h * rows:(h + 1) * rows], vpad[:, h * ATT_DV:(h + 1) * ATT_DV],
                         preferred_element_type=F32) for h in range(ATT_HEADS)], axis=0)
        _online_softmax_step(s, m_ref, l_ref, acc_ref, pv)

        o = acc_ref[...] / l_ref[...]
        lam = _lambda_value(lq1_ref, lk1_ref, lq2_ref, lk2_ref, lam_init)
        for h in range(ATT_HEADS):
            o1 = o[h * rows:h * rows + t_new]
            o2 = o[h * rows + t_new:(h + 1) * rows]
            o_ref[:, h * ATT_DV:(h + 1) * ATT_DV] = _diff_combine_subln(o1, o2, lam, gsub_ref, lam_init)


def _attn_sample(u, kt_cache, v_cache, page_table, lams, g_subln, lam_init, t_new):
    n_batch, n_pages = page_table.shape
    assert t_new & (t_new - 1) == 0 and ATT_HEADS * 2 * t_new == LANES
    kern = functools.partial(_attn_sample_kernel, t_new=t_new, n_pages=n_pages, lam_init=lam_init)
    small = pl.BlockSpec((1, ATT_DK), lambda b, p, pt: (0, 0))
    page = lambda b, p, pt: pt[b * n_pages + jnp.minimum(p, n_pages - 1)]
    vmem = 2 * (2 * PAGE_SIZE * ATT_WIDTH * 4 + 4 * t_new * ATT_WIDTH * 4) + 4 * LANES * LANES * 4 \
        + 8 * PAGE_SIZE * ATT_WIDTH * 4
    grid_spec = pltpu.PrefetchScalarGridSpec(
        num_scalar_prefetch=1,
        grid=(n_batch, n_pages + 1),
        in_specs=[small, small, small, small,
                  pl.BlockSpec((1, ATT_DV), lambda b, p, pt: (0, 0)),
                  pl.BlockSpec((t_new, ATT_WIDTH), lambda b, p, pt: (b, 0)),
                  pl.BlockSpec((t_new, ATT_WIDTH), lambda b, p, pt: (b, 1)),
                  pl.BlockSpec((t_new, ATT_WIDTH), lambda b, p, pt: (b, 2)),
                  pl.BlockSpec((None, ATT_HEADS, 2, ATT_DK, PAGE_SIZE),
                               lambda b, p, pt: (page(b, p, pt), 0, 0, 0, 0)),
                  pl.BlockSpec((None, PAGE_SIZE * ATT_HEADS, ATT_DV),
                               lambda b, p, pt: (page(b, p, pt), 0, 0))],
        out_specs=pl.BlockSpec((t_new, ATT_WIDTH), lambda b, p, pt: (b, 0)),
        scratch_shapes=[pltpu.VMEM((LANES, ATT_DV), F32), pltpu.VMEM((LANES, LANES), F32),
                        pltpu.VMEM((LANES, LANES), F32), pltpu.VMEM((LANES, ATT_DV), F32)],
    )
    return pl.pallas_call(
        kern,
        grid_spec=grid_spec,
        out_shape=jax.ShapeDtypeStruct((n_batch * t_new, ATT_WIDTH), F32),
        compiler_params=_cparams(("parallel", "arbitrary"), vmem),
        name="attn_sample",
    )(page_table.reshape(-1), *lams, g_subln, u, u, u, kt_cache, v_cache)


def _ssd_kernel(z_ref, xbc_ref, dt_ref, cs_ref, h0_ref, cw_ref, cb_ref, dtb_ref, alog_ref, dskip_ref, gssm_ref,
                y_ref, nconv_ref, nssm_ref, h_ref, tail_ref, xc_ref, ybuf_ref, *, lb, has_state):
    q = SSD_CHUNK
    c = pl.program_id(1)
    tail_rows = SUBLANES
    taps = SSM_CONV

    @pl.when(c == 0)
    def _():
        tail_ref[...] = jnp.zeros(tail_ref.shape, F32)
        if has_state:
            tail_ref[tail_rows - (taps - 1):tail_rows, :] = cs_ref[...]
            h_ref[...] = h0_ref[...]
        else:
            h_ref[...] = jnp.zeros(h_ref.shape, F32)

    row_q = lax.broadcasted_iota(jnp.int32, (q, LANES), 0)
    valid = row_q < lb

    cw = 512
    for cc in range(XBC // cw):
        cols = slice(cc * cw, (cc + 1) * cw)
        xq = xbc_ref[:, cols]
        if lb < q:
            xq = jnp.concatenate([xq, jnp.zeros((q - lb, cw), F32)], axis=0)
        xpad = jnp.concatenate([tail_ref[:, cols], xq], axis=0)
        acc = jnp.zeros((q, cw), F32) + cb_ref[:, cols]
        for j in range(taps):
            off = tail_rows - (taps - 1) + j
            acc = acc + xpad[off:off + q, :] * cw_ref[j:j + 1, cols]
        xc = _silu(acc)
        xc_ref[:, cols] = jnp.where(jnp.tile(valid, (1, cw // LANES)), xc, 0.0)

    tail_ref[...] = xbc_ref[lb - tail_rows:lb, :]

    @pl.when(c == pl.num_programs(1) - 1)
    def _():
        nconv_ref[...] = xbc_ref[lb - (taps - 1):lb, :]

    x_dt = dt_ref[...]
    if lb < q:
        x_dt = jnp.concatenate([x_dt, jnp.zeros((q - lb, LANES), F32)], axis=0)
    x_dt = x_dt + dtb_ref[...]
    dt = jnp.maximum(x_dt, 0.0) + jnp.log1p(jnp.exp(-jnp.abs(x_dt)))
    dt = jnp.where(valid, dt, 0.0)
    a_head = -jnp.exp(alog_ref[...])
    tri_r = lax.broadcasted_iota(jnp.int32, (q, q), 0)
    tri_c = lax.broadcasted_iota(jnp.int32, (q, q), 1)
    causal = tri_c <= tri_r
    a_cum = jnp.dot(causal.astype(F32), dt * a_head, precision=lax.Precision.HIGHEST,
                    preferred_element_type=F32)
    a_cum_t = a_cum.T
    dt_t = dt.T
    a_last = a_cum[q - 1:q, :]
    to_end = jnp.exp(a_last - a_cum) * dt
    exp_a = jnp.exp(a_cum)
    exp_last = jnp.exp(a_last)

    lane = lax.broadcasted_iota(jnp.int32, (q, LANES), 1)
    first = lane < SSM_HEAD_DIM
    heads_per_group = SSM_HEADS // SSM_GROUPS
    b_off = SSM_INNER
    c_off = SSM_INNER + SSM_GROUPS * SSM_STATE
    cb_g = None
    for pair in range(SSM_HEADS // 2):
        h_a, h_b = 2 * pair, 2 * pair + 1
        g = h_a // heads_per_group
        bm = xc_ref[:, b_off + g * SSM_STATE:b_off + (g + 1) * SSM_STATE]
        cm = xc_ref[:, c_off + g * SSM_STATE:c_off + (g + 1) * SSM_STATE]
        bm_b, cm_b = bm.astype(BF16), cm.astype(BF16)
        if h_a % heads_per_group == 0:
            cb_g = _nt_dot(cm_b, bm_b)
        x_pair = xc_ref[:, pair * LANES:(pair + 1) * LANES]
        x_pair_b = x_pair.astype(BF16)

        def intra(hh):
            seg = a_cum[:, hh:hh + 1] - a_cum_t[hh:hh + 1, :]
            decay = jnp.exp(jnp.where(causal, seg, NEG_INF))
            w = cb_g * decay * dt_t[hh:hh + 1, :]
            return jnp.dot(w.astype(BF16), x_pair_b, preferred_element_type=F32)
        y_pair = jnp.where(first, intra(h_a), intra(h_b))

        h_pair = h_ref[h_a:h_b + 1].reshape(2 * SSM_HEAD_DIM, SSM_STATE)
        inter = _nt_dot(cm_b, h_pair.astype(BF16))
        sel = lambda v: jnp.where(first, v[:, h_a:h_a + 1], v[:, h_b:h_b + 1])
        y_pair = y_pair + inter * sel(exp_a)
        y_pair = y_pair + x_pair * sel(dskip_ref[...])
        ybuf_ref[:, pair * LANES:(pair + 1) * LANES] = y_pair

        xw = (x_pair * sel(to_end)).T.astype(BF16)
        upd = jnp.dot(xw, bm_b, preferred_element_type=F32)
        row = lax.broadcasted_iota(jnp.int32, (2 * SSM_HEAD_DIM, SSM_STATE), 0)
        keep = jnp.where(row < SSM_HEAD_DIM, exp_last[:, h_a:h_a + 1], exp_last[:, h_b:h_b + 1])
        h_ref[h_a:h_b + 1] = (h_pair * keep + upd).reshape(2, SSM_HEAD_DIM, SSM_STATE)

    @pl.when(c == pl.num_programs(1) - 1)
    def _():
        nssm_ref[...] = h_ref[...]

    gw = SSM_INNER // SSM_GROUPS
    for g in range(SSM_GROUPS):
        cols = slice(g * gw, (g + 1) * gw)
        yg = ybuf_ref[0:lb, cols] * _silu(z_ref[:, cols])
        ms = jnp.mean(yg * yg, axis=-1, keepdims=True)
        y_ref[:, cols] = yg * lax.rsqrt(ms + EPS) * gssm_ref[:, cols]


def _ssd(u, dt_raw, conv_state, ssm_state, conv_w, conv_b, dt_bias, a_log, d_skip, g_ssm, n_seq, seq_len):
    lb = min(SSD_CHUNK, seq_len)
    n_chunks = seq_len // lb
    has_state = conv_state is not None
    if not has_state:
        conv_state = jnp.zeros((n_seq, SSM_CONV - 1, XBC), F32)
        ssm_state = jnp.zeros((n_seq, SSM_HEADS, SSM_HEAD_DIM, SSM_STATE), F32)
    rows = n_seq * seq_len
    z_blk = (3 * ATT_WIDTH) // SSM_INNER
    xbc_blk = (3 * ATT_WIDTH + SSM_INNER) // XBC
    blk = lambda s, c: s * n_chunks + c
    vec = lambda n: pl.BlockSpec((1, n), lambda s, c: (0, 0))
    kern = functools.partial(_ssd_kernel, lb=lb, has_state=has_state)
    state_bytes = SSM_HEADS * SSM_HEAD_DIM * SSM_STATE * 4
    vmem = 2 * (lb * (SSM_INNER + XBC + LANES + SSM_INNER) * 4 + 2 * state_bytes) + state_bytes \
        + SSD_CHUNK * (XBC + SSM_INNER) * 4 + 16 * 1024 * 1024
    return pl.pallas_call(
        kern,
        grid=(n_seq, n_chunks),
        in_specs=[
            pl.BlockSpec((lb, SSM_INNER), lambda s, c: (blk(s, c), z_blk)),
            pl.BlockSpec((lb, XBC), lambda s, c: (blk(s, c), xbc_blk)),
            pl.BlockSpec((lb, LANES), lambda s, c: (blk(s, c), 0)),
            pl.BlockSpec((None, SSM_CONV - 1, XBC), lambda s, c: (s, 0, 0)),
            pl.BlockSpec((None, SSM_HEADS, SSM_HEAD_DIM, SSM_STATE), lambda s, c: (s, 0, 0, 0)),
            pl.BlockSpec((SSM_CONV, XBC), lambda s, c: (0, 0)),
            vec(XBC), vec(LANES), vec(LANES), vec(LANES), vec(SSM_INNER),
        ],
        out_specs=[
            pl.BlockSpec((lb, SSM_INNER), lambda s, c: (blk(s, c), 0)),
            pl.BlockSpec((None, SSM_CONV - 1, XBC), lambda s, c: (s, 0, 0)),
            pl.BlockSpec((None, SSM_HEADS, SSM_HEAD_DIM, SSM_STATE), lambda s, c: (s, 0, 0, 0)),
        ],
        out_shape=[jax.ShapeDtypeStruct((rows, SSM_INNER), F32),
                   jax.ShapeDtypeStruct((n_seq, SSM_CONV - 1, XBC), F32),
                   jax.ShapeDtypeStruct((n_seq, SSM_HEADS, SSM_HEAD_DIM, SSM_STATE), F32)],
        scratch_shapes=[pltpu.VMEM((SSM_HEADS, SSM_HEAD_DIM, SSM_STATE), F32),
                        pltpu.VMEM((SUBLANES, XBC), F32),
                        pltpu.VMEM((SSD_CHUNK, XBC), F32),
                        pltpu.VMEM((SSD_CHUNK, SSM_INNER), F32)],
        compiler_params=_cparams(("parallel", "arbitrary"), vmem),
        name="ssd",
    )(u, u, dt_raw, conv_state, ssm_state, conv_w, conv_b, dt_bias, a_log, d_skip, g_ssm)


def _proj_res_norm_kernel(*refs, k_sizes, tn):
    n_in = len(k_sizes)
    a_refs = refs[:n_in]
    w_ref, xres_ref, g_ref, xo_ref, xn_ref = refs[n_in:]
    tm = xo_ref.shape[0]
    a_vals = [a_ref[...].astype(BF16) for a_ref in a_refs]
    ssq = jnp.zeros((tm, 1), F32)
    for n in range(D_MODEL // tn):
        cols = slice(n * tn, (n + 1) * tn)
        acc = xres_ref[:, cols]
        k0 = 0
        for a, ks in zip(a_vals, k_sizes):
            acc = acc + jnp.dot(a, w_ref[k0:k0 + ks, cols], preferred_element_type=F32)
            k0 += ks
        xo_ref[:, cols] = acc
        ssq = ssq + jnp.sum(acc * acc, axis=-1, keepdims=True)
    inv = lax.rsqrt(ssq * (1.0 / D_MODEL) + EPS)
    for n in range(D_MODEL // tn):
        cols = slice(n * tn, (n + 1) * tn)
        xn_ref[:, cols] = (xo_ref[:, cols] * inv * g_ref[:, cols]).astype(xn_ref.dtype)


def _proj_res_norm(a_list, w, xres, g, xn_dtype, tm):
    r = xres.shape[0]
    k_sizes = tuple(a.shape[1] for a in a_list)
    kern = functools.partial(_proj_res_norm_kernel, k_sizes=k_sizes, tn=512)
    a_bytes = sum(tm * a.shape[1] * a.dtype.itemsize for a in a_list)
    vmem = 2 * (a_bytes + D_MODEL * D_MODEL * 2 + 3 * tm * D_MODEL * 4) + 8 * tm * 512 * 4
    return pl.pallas_call(
        kern,
        grid=(r // tm,),
        in_specs=[pl.BlockSpec((tm, a.shape[1]), lambda i: (i, 0)) for a in a_list] + [
            pl.BlockSpec((D_MODEL, D_MODEL), lambda i: (0, 0)),
            pl.BlockSpec((tm, D_MODEL), lambda i: (i, 0)),
            pl.BlockSpec((1, D_MODEL), lambda i: (0, 0)),
        ],
        out_specs=[pl.BlockSpec((tm, D_MODEL), lambda i: (i, 0)),
                   pl.BlockSpec((tm, D_MODEL), lambda i: (i, 0))],
        out_shape=[jax.ShapeDtypeStruct((r, D_MODEL), F32), jax.ShapeDtypeStruct((r, D_MODEL), xn_dtype)],
        compiler_params=_cparams(("parallel",), vmem),
        name="proj_res_norm",
    )(*a_list, w, xres, g)


def _matmul_kernel(a_ref, w_ref, o_ref, *, tn):
    a = a_ref[...].astype(BF16)
    for n in range(o_ref.shape[1] // tn):
        cols = slice(n * tn, (n + 1) * tn)
        o_ref[:, cols] = jnp.dot(a, w_ref[:, cols], preferred_element_type=F32).astype(o_ref.dtype)


def _matmul(a, w, out_dtype, tm):
    r, k = a.shape
    n = w.shape[1]
    vmem = 2 * (tm * k * a.dtype.itemsize + k * n * 2 + tm * n * 4) + 4 * tm * 512 * 4
    return pl.pallas_call(
        functools.partial(_matmul_kernel, tn=512),
        grid=(r // tm,),
        in_specs=[pl.BlockSpec((tm, k), lambda i: (i, 0)), pl.BlockSpec((k, n), lambda i: (0, 0))],
        out_specs=pl.BlockSpec((tm, n), lambda i: (i, 0)),
        out_shape=jax.ShapeDtypeStruct((r, n), out_dtype),
        compiler_params=_cparams(("parallel",), vmem),
        name="matmul",
    )(a, w)


def _memattn_kernel(q_ref, k_ref, v_ref, o_ref, kb_ref, vb_ref):
    @pl.when(pl.program_id(1) == 0)
    def _():
        kb_ref[...] = k_ref[...].astype(BF16)
        vb_ref[...] = v_ref[...].astype(BF16)

    scale = MEM_DH ** -0.5
    for h in range(MEM_HEADS):
        cols = slice(h * MEM_DH, (h + 1) * MEM_DH)
        s = _nt_dot(q_ref[:, cols].astype(BF16), kb_ref[:, cols]) * scale
        p = jnp.exp(s - jnp.max(s, axis=1, keepdims=True))
        l = jnp.sum(p, axis=1, keepdims=True)
        o = jnp.dot(p.astype(BF16), vb_ref[:, cols], preferred_element_type=F32) / l
        o_ref[:, cols] = o.astype(o_ref.dtype)


def _mem_attention(qm, mem_k, mem_v, out_dtype, tq):
    n_batch = mem_k.shape[0]
    per_b = qm.shape[0] // n_batch
    nq = per_b // tq
    vmem = 2 * (2 * MEM_TOKENS * D_MODEL * 4 + tq * D_MODEL * (qm.dtype.itemsize + 4)) \
        + 2 * MEM_TOKENS * D_MODEL * 2 + 8 * tq * MEM_DH * 4
    return pl.pallas_call(
        _memattn_kernel,
        grid=(n_batch, nq),
        in_specs=[pl.BlockSpec((tq, D_MODEL), lambda b, i: (b * nq + i, 0)),
                  pl.BlockSpec((None, MEM_TOKENS, D_MODEL), lambda b, i: (b, 0, 0)),
                  pl.BlockSpec((None, MEM_TOKENS, D_MODEL), lambda b, i: (b, 0, 0))],
        out_specs=pl.BlockSpec((tq, D_MODEL), lambda b, i: (b * nq + i, 0)),
        out_shape=jax.ShapeDtypeStruct(qm.shape, out_dtype),
        scratch_shapes=[pltpu.VMEM((MEM_TOKENS, D_MODEL), BF16), pltpu.VMEM((MEM_TOKENS, D_MODEL), BF16)],
        compiler_params=_cparams(("parallel", "arbitrary"), vmem),
        name="mem_attention",
    )(qm, mem_k, mem_v)


def _router_kernel(x_ref, whi_ref, wlo_ref, e_ref, g_ref):
    x = x_ref[...].astype(BF16)
    logits = (jnp.dot(x, whi_ref[...], preferred_element_type=F32)
              + jnp.dot(x, wlo_ref[...], preferred_element_type=F32))
    lane = lax.broadcasted_iota(jnp.int32, logits.shape, 1)
    lane_f = lane.astype(F32)
    big = float(LANES)

    def first_argmax(v):
        m = jnp.max(v, axis=1, keepdims=True)
        idx = jnp.min(jnp.where(v == m, lane_f, big), axis=1, keepdims=True)
        return m, idx

    lg = jnp.where(lane < N_GROUPS, logits, NEG_INF)
    mg, grp = first_argmax(lg)
    gate_grp = 1.0 / jnp.sum(jnp.exp(lg - mg), axis=1, keepdims=True)
    lo = N_GROUPS + grp.astype(jnp.int32) * EXPERTS_PER_GROUP
    le = jnp.where((lane >= lo) & (lane < lo + EXPERTS_PER_GROUP), logits, NEG_INF)
    m1, i1 = first_argmax(le)
    le2 = jnp.where(lane_f == i1, NEG_INF, le)
    m2, i2 = first_argmax(le2)
    e2 = jnp.exp(m2 - m1)
    g1 = gate_grp / (1.0 + e2)
    g2 = gate_grp * e2 / (1.0 + e2)
    ex1 = i1.astype(jnp.int32) - N_GROUPS
    ex2 = i2.astype(jnp.int32) - N_GROUPS
    e_ref[...] = jnp.where(lane == 0, ex1, jnp.where(lane == 1, ex2, 0))
    g_ref[...] = jnp.where(lane == 0, g1, jnp.where(lane == 1, g2, 0.0))


def _router(xn, w_hi, w_lo, tm):
    r = xn.shape[0]
    vmem = 2 * (tm * D_MODEL * 4 + 2 * D_MODEL * LANES * 2 + 2 * tm * LANES * 4) + 16 * tm * LANES * 4
    return pl.pallas_call(
        _router_kernel,
        grid=(r // tm,),
        in_specs=[pl.BlockSpec((tm, D_MODEL), lambda i: (i, 0)),
                  pl.BlockSpec((D_MODEL, LANES), lambda i: (0, 0)),
                  pl.BlockSpec((D_MODEL, LANES), lambda i: (0, 0))],
        out_specs=[pl.BlockSpec((tm, LANES), lambda i: (i, 0)), pl.BlockSpec((tm, LANES), lambda i: (i, 0))],
        out_shape=[jax.ShapeDtypeStruct((r, LANES), jnp.int32), jax.ShapeDtypeStruct((r, LANES), F32)],
        compiler_params=_cparams(("parallel",), vmem),
        name="router",
    )(xn, w_hi, w_lo)


def _rank_kernel(e_ref, rank_ref, cnt_ref, carry_ref):
    i = pl.program_id(0)
    tm = e_ref.shape[0]

    @pl.when(i == 0)
    def _():
        carry_ref[...] = jnp.zeros(carry_ref.shape, F32)

    e = e_ref[...].astype(F32)
    lane = lax.broadcasted_iota(jnp.int32, e.shape, 1)
    lane_f = lane.astype(F32)
    e1 = jnp.sum(jnp.where(lane == 0, e, 0.0), axis=1, keepdims=True)
    e2 = jnp.sum(jnp.where(lane == 1, e, 0.0), axis=1, keepdims=True)
    oh1 = lane_f == e1
    oh2 = lane_f == e2
    oh = jnp.where(oh1 | oh2, 1.0, 0.0)
    r_i = lax.broadcasted_iota(jnp.int32, (tm, tm), 0)
    c_i = lax.broadcasted_iota(jnp.int32, (tm, tm), 1)
    before = jnp.dot(jnp.where(c_i < r_i, 1.0, 0.0).astype(BF16), oh.astype(BF16),
                     preferred_element_type=F32) + carry_ref[...]
    rank1 = jnp.sum(jnp.where(oh1, before, 0.0), axis=1, keepdims=True)
    rank2 = jnp.sum(jnp.where(oh2, before, 0.0), axis=1, keepdims=True)
    rank_ref[...] = jnp.where(lane == 0, rank1, jnp.where(lane == 1, rank2, 0.0)).astype(jnp.int32)
    carry_ref[...] = carry_ref[...] + jnp.sum(oh, axis=0, keepdims=True)

    @pl.when(i == pl.num_programs(0) - 1)
    def _():
        cnt_ref[...] = jnp.tile(carry_ref[...], (SUBLANES, 1)).astype(jnp.int32)


def _expert_ranks(e_lanes, tm):
    r = e_lanes.shape[0]
    return pl.pallas_call(
        _rank_kernel,
        grid=(r // tm,),
        in_specs=[pl.BlockSpec((tm, LANES), lambda i: (i, 0))],
        out_specs=[pl.BlockSpec((tm, LANES), lambda i: (i, 0)), pl.BlockSpec((SUBLANES, LANES), lambda i: (0, 0))],
        out_shape=[jax.ShapeDtypeStruct((r, LANES), jnp.int32), jax.ShapeDtypeStruct((SUBLANES, LANES), jnp.int32)],
        scratch_shapes=[pltpu.VMEM((1, LANES), F32)],
        compiler_params=_cparams(("arbitrary",), 16 * 1024 * 1024),
        name="expert_ranks",
    )(e_lanes)


def _rowmap_kernel(e_ref, rank_ref, start_ref, rowtok_ref, pos0_ref, pos1_ref, *, n_tok, n_rows, pad_tok):
    def init(i, carry):
        rowtok_ref[i] = pad_tok
        return carry
    lax.fori_loop(0, n_rows, init, 0)

    def body(t, carry):
        a0 = 2 * t
        p0 = start_ref[e_ref[a0]] + rank_ref[a0]
        p1 = start_ref[e_ref[a0 + 1]] + rank_ref[a0 + 1]
        pos0_ref[t] = p0
        pos1_ref[t] = p1
        rowtok_ref[p0] = t
        rowtok_ref[p1] = t
        return carry
    lax.fori_loop(0, n_tok, body, 0)


def _row_maps(e_flat, rank_flat, pad_start, n_rows, pad_tok):
    n_tok = e_flat.shape[0] // TOP_K
    smem = pl.BlockSpec(memory_space=pltpu.SMEM)
    return pl.pallas_call(
        functools.partial(_rowmap_kernel, n_tok=n_tok, n_rows=n_rows, pad_tok=pad_tok),
        in_specs=[smem, smem, smem],
        out_specs=[smem, smem, smem],
        out_shape=[jax.ShapeDtypeStruct((n_rows,), jnp.int32), jax.ShapeDtypeStruct((n_tok,), jnp.int32),
                   jax.ShapeDtypeStruct((n_tok,), jnp.int32)],
        name="row_maps",
    )(e_flat, rank_flat, pad_start)


GATHER_WINDOW = 16


def _gather_kernel(idx_ref, src_ref, out_ref, sem, *, n):
    def copy(i, row):
        return pltpu.make_async_copy(src_ref.at[pl.ds(row, 1), :], out_ref.at[pl.ds(i, 1), :], sem)

    def prologue(i, carry):
        copy(i, idx_ref[i]).start()
        return carry
    lax.fori_loop(0, GATHER_WINDOW, prologue, 0)

    def steady(i, carry):
        copy(0, 0).wait()
        copy(i, idx_ref[i]).start()
        return carry
    lax.fori_loop(GATHER_WINDOW, n, steady, 0)

    def drain(i, carry):
        copy(0, 0).wait()
        return carry
    lax.fori_loop(0, GATHER_WINDOW, drain, 0)


def _gather_rows(idx, src):
    n = idx.shape[0]
    assert n >= GATHER_WINDOW
    grid_spec = pltpu.PrefetchScalarGridSpec(
        num_scalar_prefetch=1, grid=(1,),
        in_specs=[pl.BlockSpec(memory_space=pl.ANY)],
        out_specs=pl.BlockSpec(memory_space=pl.ANY),
        scratch_shapes=[pltpu.SemaphoreType.DMA(())],
    )
    return pl.pallas_call(
        functools.partial(_gather_kernel, n=n),
        grid_spec=grid_spec,
        out_shape=jax.ShapeDtypeStruct((n, src.shape[1]), src.dtype),
        compiler_params=pltpu.CompilerParams(dimension_semantics=("arbitrary",)),
        name="gather_rows",
    )(idx, src)


def _moe_kernel(be_ref, nb_ref, x_ref, wg_ref, wu_ref, wd_ref, o_ref):
    del be_ref
    i = pl.program_id(0)

    @pl.when(i < nb_ref[0])
    def _():
        x = x_ref[...].astype(BF16)
        gate = jnp.dot(x, wg_ref[...], preferred_element_type=F32)
        up = jnp.dot(x, wu_ref[...], preferred_element_type=F32)
        hid = (_silu(gate) * up).astype(BF16)
        o_ref[...] = jnp.dot(hid, wd_ref[...], preferred_element_type=F32)

    @pl.when(i >= nb_ref[0])
    def _():
        o_ref[...] = jnp.zeros(o_ref.shape, F32)


def _moe_experts(xs, blk_expert, n_used, w_gate, w_up, w_down):
    rows = xs.shape[0]
    n_blocks = rows // MOE_BLOCK
    wbytes = D_MODEL * D_EXPERT * 2
    vmem = 2 * (3 * wbytes + 2 * MOE_BLOCK * D_MODEL * 4) + 6 * MOE_BLOCK * D_EXPERT * 4
    grid_spec = pltpu.PrefetchScalarGridSpec(
        num_scalar_prefetch=2, grid=(n_blocks,),
        in_specs=[pl.BlockSpec((MOE_BLOCK, D_MODEL), lambda i, be, nb: (i, 0)),
                  pl.BlockSpec((None, D_MODEL, D_EXPERT), lambda i, be, nb: (be[i], 0, 0)),
                  pl.BlockSpec((None, D_MODEL, D_EXPERT), lambda i, be, nb: (be[i], 0, 0)),
                  pl.BlockSpec((None, D_EXPERT, D_MODEL), lambda i, be, nb: (be[i], 0, 0))],
        out_specs=pl.BlockSpec((MOE_BLOCK, D_MODEL), lambda i, be, nb: (i, 0)),
    )
    return pl.pallas_call(
        _moe_kernel,
        grid_spec=grid_spec,
        out_shape=jax.ShapeDtypeStruct((rows, D_MODEL), F32),
        compiler_params=_cparams(("arbitrary",), vmem),
        name="moe_experts",
    )(blk_expert, n_used, xs, w_gate, w_up, w_down)


def _combine_kernel(x_ref, r0_ref, r1_ref, g_ref, gf_ref, o_ref):
    lane = lax.broadcasted_iota(jnp.int32, g_ref.shape, 1)
    g = g_ref[...]
    g0 = jnp.sum(jnp.where(lane == 0, g, 0.0), axis=1, keepdims=True)
    g1 = jnp.sum(jnp.where(lane == 1, g, 0.0), axis=1, keepdims=True)
    y = x_ref[...] + (g0 * r0_ref[...] + g1 * r1_ref[...])
    ms = jnp.mean(y * y, axis=-1, keepdims=True)
    o_ref[...] = y * lax.rsqrt(ms + EPS) * gf_ref[...]


def _combine_final(x, r0, r1, gates, g_final, row0, tm):
    r = x.shape[0]
    off = row0 // tm
    row = pl.BlockSpec((tm, D_MODEL), lambda i: (i, 0))
    row_off = pl.BlockSpec((tm, D_MODEL), lambda i: (off + i, 0))
    vmem = 2 * (4 * tm * D_MODEL * 4 + tm * LANES * 4) + 4 * tm * D_MODEL * 4
    return pl.pallas_call(
        _combine_kernel,
        grid=(r // tm,),
        in_specs=[row, row_off, row_off, pl.BlockSpec((tm, LANES), lambda i: (off + i, 0)),
                  pl.BlockSpec((1, D_MODEL), lambda i: (0, 0))],
        out_specs=row,
        out_shape=jax.ShapeDtypeStruct((r, D_MODEL), F32),
        compiler_params=_cparams(("parallel",), vmem),
        name="combine_final",
    )(x, r0, r1, gates, g_final)


def _pad_lanes(v):
    return jnp.pad(v.astype(F32), (0, LANES - v.shape[0])).reshape(1, LANES)


def _mixer_block(x, pos, attn_fn, ssd_state, lw, n_seq, seq_len, tm):
    u, dt_raw = _in_projection(x, lw["g_mix"], lw["w_in_main"], lw["w_in_dt"], pos, tm)
    o_att = attn_fn(u)
    conv_state, ssm_state = ssd_state
    o_ssm, new_conv, new_ssm = _ssd(u, dt_raw, conv_state, ssm_state, lw["conv_w"], lw["conv_b"], lw["dt_bias"],
                                    lw["a_log"], lw["d_skip"], lw["g_ssm"], n_seq, seq_len)
    x1, xn1 = _proj_res_norm([o_att, o_ssm], lw["w_out"], x, lw["g_cross"], BF16, min(tm, 256))
    k_new = u[:, ATT_WIDTH:2 * ATT_WIDTH]
    v_new = u[:, 2 * ATT_WIDTH:3 * ATT_WIDTH]
    return x1, xn1, k_new, v_new, new_conv, new_ssm


def _cross_block(x1, xn1, mem_k, mem_v, lw, tq, tm):
    qm_dtype = BF16 if tq % 16 == 0 else F32
    qm = _matmul(xn1, lw["w_mq"], qm_dtype, tm)
    o_mem = _mem_attention(qm, mem_k, mem_v, qm_dtype, tq)
    return _proj_res_norm([o_mem], lw["w_mo"], x1, lw["g_ffn"], F32, min(tm, 256))


def kernel(x_prompt, x_sample, mem_prompt, cache_attn_k, cache_attn_v, page_table, cache_mem_k, cache_mem_v, state_conv, state_ssm, g_mix, w_in, lambda_q1, lambda_k1, lambda_q2, lambda_k2, g_subln, conv_w, conv_b, dt_bias, a_log, d_skip, g_ssm, w_out, g_cross, g_mem, w_mq, w_mk, w_mv, w_mo, g_ffn, w_router_group, w_router_expert, w_gate, w_up, w_down, g_final):
    n_b, seq, _ = x_prompt.shape
    n_db, t_new, _ = x_sample.shape
    depth = g_mix.shape[0]
    assert n_b == 1 and depth == 1
    l = 0
    lam_init = 0.8 - 0.6 * math.exp(-0.3 * l)
    past_len = page_table.shape[1] * PAGE_SIZE
    r_p, r_s = n_b * seq, n_db * t_new

    row = lambda v: v.reshape(1, -1).astype(F32)
    w_in_l = w_in[l]
    lw = dict(
        g_mix=row(g_mix[l]), g_cross=row(g_cross[l]), g_ffn=row(g_ffn[l]),
        w_in_main=w_in_l[:, :IN_MAIN].astype(BF16),
        w_in_dt=jnp.pad(w_in_l[:, IN_MAIN:], ((0, 0), (0, LANES - SSM_HEADS))).astype(BF16),
        conv_w=conv_w[l], conv_b=row(conv_b[l]), dt_bias=_pad_lanes(dt_bias[l]), a_log=_pad_lanes(a_log[l]),
        d_skip=_pad_lanes(d_skip[l]), g_ssm=row(g_ssm[l]),
        w_out=w_out[l].astype(BF16), w_mq=w_mq[l].astype(BF16), w_mo=w_mo[l].astype(BF16),
    )
    lams = [row(lambda_q1[l]), row(lambda_k1[l]), row(lambda_q2[l]), row(lambda_k2[l])]
    gsub = row(g_subln[l])

    w_mkv = jnp.concatenate([w_mk[l], w_mv[l]], axis=1).astype(BF16)
    mkv = _norm_matmul(mem_prompt.reshape(MEM_TOKENS, D_MODEL), row(g_mem[l]), w_mkv, MEM_TOKENS, 1024)
    mk_p, mv_p = mkv[:, :D_MODEL], mkv[:, D_MODEL:]

    xp = x_prompt.reshape(r_p, D_MODEL)
    attn_p = lambda u: _attn_prompt(u, lams, gsub, lam_init, 512)
    x1p, xn1p, k_p, v_p, conv_p, ssm_p = _mixer_block(xp, jnp.arange(seq), attn_p, (None, None), lw, n_b, seq, 1024)
    x2p, xn2p = _cross_block(x1p, xn1p, mk_p[None], mv_p[None], lw, 512, 512)

    xs = x_sample.reshape(r_s, D_MODEL)
    kt_cache = jnp.transpose(cache_attn_k[l], (0, 2, 3, 4, 1))
    v_cache = cache_attn_v[l].reshape(-1, PAGE_SIZE * ATT_HEADS, ATT_DV)
    pos_s = jnp.tile(past_len + jnp.arange(t_new), n_db)
    attn_s = lambda u: _attn_sample(u, kt_cache, v_cache, page_table, lams, gsub, lam_init, t_new)
    x1s, xn1s, k_s, v_s, conv_s, ssm_s = _mixer_block(xs, pos_s, attn_s, (state_conv[l], state_ssm[l]), lw,
                                                      n_db, t_new, 512)
    x2s, xn2s = _cross_block(x1s, xn1s, cache_mem_k[l].reshape(n_db, MEM_TOKENS, D_MODEL),
                             cache_mem_v[l].reshape(n_db, MEM_TOKENS, D_MODEL), lw, t_new, 512)

    n_tok = r_p + r_s
    w_r = jnp.concatenate([w_router_group[l], w_router_expert[l],
                           jnp.zeros((D_MODEL, LANES - N_GROUPS - N_EXPERTS), F32)], axis=1)
    w_r_hi = w_r.astype(BF16)
    w_r_lo = (w_r - w_r_hi.astype(F32)).astype(BF16)
    e_p, g_p = _router(xn2p, w_r_hi, w_r_lo, 512)
    e_s, g_s = _router(xn2s, w_r_hi, w_r_lo, 512)
    e_all = jnp.concatenate([e_p, e_s], axis=0)
    gates = jnp.concatenate([g_p, g_s], axis=0)
    rank_all, cnt = _expert_ranks(e_all, 512)
    counts = cnt[0, :N_EXPERTS]
    padded = (counts + MOE_BLOCK - 1) // MOE_BLOCK * MOE_BLOCK
    pad_end = jnp.cumsum(padded)
    pad_start = (pad_end - padded).astype(jnp.int32)
    n_blocks = n_tok * TOP_K // MOE_BLOCK + N_EXPERTS
    n_rows = n_blocks * MOE_BLOCK
    blk_start = jnp.arange(n_blocks, dtype=jnp.int32) * MOE_BLOCK
    blk_expert = jnp.minimum(jnp.sum(blk_start[:, None] >= pad_end[None, :], axis=1), N_EXPERTS - 1).astype(jnp.int32)
    n_used = (pad_end[-1:] // MOE_BLOCK).astype(jnp.int32)
    row_tok, pos0, pos1 = _row_maps(e_all[:, :TOP_K].reshape(-1), rank_all[:, :TOP_K].reshape(-1), pad_start,
                                    n_rows, n_tok)
    x_ext = jnp.concatenate([xn2p, xn2s, jnp.zeros((SUBLANES, D_MODEL), F32)], axis=0)
    xs_sorted = _gather_rows(row_tok, x_ext)
    yb = _moe_experts(xs_sorted, blk_expert, n_used, w_gate[l].astype(BF16), w_up[l].astype(BF16),
                      w_down[l].astype(BF16))
    r0 = _gather_rows(pos0, yb)
    r1 = _gather_rows(pos1, yb)
    gf = row(g_final)
    y_p = _combine_final(x2p, r0, r1, gates, gf, 0, 256)
    y_s = _combine_final(x2s, r0, r1, gates, gf, r_p, 256)

    return (y_p.reshape(n_b, seq, D_MODEL), y_s.reshape(n_db, t_new, D_MODEL),
            k_p.reshape(1, n_b, seq, ATT_HEADS, 2, ATT_DK), v_p.reshape(1, n_b, seq, ATT_HEADS, ATT_DV),
            k_s.reshape(1, n_db, t_new, ATT_HEADS, 2, ATT_DK), v_s.reshape(1, n_db, t_new, ATT_HEADS, ATT_DV),
            conv_p[None], conv_s[None], ssm_p[None], ssm_s[None],
            mk_p.reshape(1, n_b, MEM_TOKENS, MEM_HEADS, MEM_DH), mv_p.reshape(1, n_b, MEM_TOKENS, MEM_HEADS, MEM_DH))
```
